```python
import math
import jax, jax.numpy as jnp
from jax import lax
import numpy as np

D_MODEL = 1024
BATCH = 4
SEQ = 4096
DEPTH = 4
DEC_BATCH = 32
DEC_SEQ = 4
PAST_LEN = 8192
PAGE_SIZE = 128

SSM_HEADS = 16
SSM_HEAD_DIM = 64
SSM_INNER = SSM_HEADS * SSM_HEAD_DIM
SSM_GROUPS = 2
SSM_STATE = 128
SSM_CONV = 4
SSM_CHUNK = 128
XBC_DIM = SSM_INNER + 2 * SSM_GROUPS * SSM_STATE
CONF_CH = 1024
CONF_WIDTH = 31
IN_AB = SSM_INNER + XBC_DIM + SSM_HEADS + 2 * CONF_CH
MIX_AB = SSM_INNER + CONF_CH
SPLIT_AB = [SSM_INNER, SSM_INNER + XBC_DIM, SSM_INNER + XBC_DIM + SSM_HEADS,
            SSM_INNER + XBC_DIM + SSM_HEADS + CONF_CH]
SB_HEADS = 16
SB_HEAD_DIM = D_MODEL // SB_HEADS
Q_BLOCK = 128
SB_BIAS_INIT = -6.0
N_EXPERTS = 32
TOP_K = 4
D_FF = D_MODEL
SWIGLU_ALPHA = 1.702
SWIGLU_LIMIT = 7.0
MOE_BLOCK = 64
EPS = 1e-6
N_AB = (DEPTH + 1) // 2
N_SB = DEPTH // 2

kernel_name = 'hybrid_ssd_conformer_stickbreak_moe_step'


def rms_norm(x, g):
    xf = x.astype(jnp.float32)
    y = xf * lax.rsqrt(jnp.mean(xf * xf, axis=-1, keepdims=True) + EPS)
    return (y * g.astype(jnp.float32)).astype(x.dtype)


def layer_norm(x, g, b):
    xf = x.astype(jnp.float32)
    mu = jnp.mean(xf, axis=-1, keepdims=True)
    xc = xf - mu
    var = jnp.mean(xc * xc, axis=-1, keepdims=True)
    return (xc * lax.rsqrt(var + EPS) * g.astype(jnp.float32) + b.astype(jnp.float32)).astype(x.dtype)


def gated_group_rms(y, z, g):
    yf = y * jax.nn.silu(z.astype(jnp.float32))
    yg = yf.reshape(yf.shape[:-1] + (SSM_GROUPS, SSM_INNER // SSM_GROUPS))
    yg = yg * lax.rsqrt(jnp.mean(yg * yg, axis=-1, keepdims=True) + EPS)
    return yg.reshape(yf.shape) * g.astype(jnp.float32)


def causal_dwconv(x_hist, w, b):
    ch = w.shape[1]
    y = lax.conv_general_dilated(x_hist, w[:, None, :].astype(x_hist.dtype), window_strides=(1,),
                                 padding='VALID', dimension_numbers=('NWC', 'WIO', 'NWC'),
                                 feature_group_count=ch)
    return y + b.astype(y.dtype)


def segsum(a):
    cs = jnp.cumsum(a, axis=-1)
    t = a.shape[-1]
    diff = cs[..., :, None] - cs[..., None, :]
    return jnp.where(jnp.tril(jnp.ones((t, t), bool)), diff, -jnp.inf)


def ssd_chunked(x, a, bm, cm, init, chunk):
    b, l, h, p = x.shape
    g, n = bm.shape[-2:]
    hg = h // g
    c = l // chunk
    x = x.reshape(b, c, chunk, g, hg, p)
    bm = bm.reshape(b, c, chunk, g, n)
    cm = cm.reshape(b, c, chunk, g, n)
    a = a.reshape(b, c, chunk, g, hg).transpose(0, 3, 4, 1, 2)
    a_cs = jnp.cumsum(a, axis=-1)
    decay_in = jnp.exp(segsum(a))
    cb = jnp.einsum('bclgn,bcsgn->bgcls', cm, bm)
    y_diag = jnp.einsum('bghcls,bcsghp->bclghp', cb[:, :, None] * decay_in, x)
    decay_states = jnp.exp(a_cs[..., -1:] - a_cs)
    states = jnp.einsum('bclgn,bghcl,bclghp->bcghpn', bm, decay_states, x)
    states = jnp.concatenate([init.reshape(b, 1, g, hg, p, n), states], axis=1)
    chunk_a = jnp.pad(a_cs[..., -1], ((0, 0), (0, 0), (0, 0), (1, 0)))
    decay_chunk = jnp.exp(segsum(chunk_a))
    states = jnp.einsum('bghzc,bcghpn->bzghpn', decay_chunk, states)
    prev_states, final = states[:, :-1], states[:, -1]
    y_off = jnp.einsum('bclgn,bcghpn,bghcl->bclghp', cm, prev_states, jnp.exp(a_cs))
    return (y_diag + y_off).reshape(b, l, h, p), final.reshape(b, h, p, n)


def ab_mixer(h, ssm0, convx0, convc0, w_in, conv_w, conv_b, dt_bias, a_log, d_skip, norm_g,
             dw_w, dw_b, ln_g, ln_b, w_out):
    b, l, _ = h.shape
    proj = h @ w_in
    z, xbc, dt_raw, glu_a, glu_b = jnp.split(proj, SPLIT_AB, axis=-1)
    xbc_hist = jnp.concatenate([convx0.astype(h.dtype), xbc], axis=1)
    new_convx = xbc_hist[:, -(SSM_CONV - 1):]
    xbc = jax.nn.silu(causal_dwconv(xbc_hist, conv_w, conv_b))
    xs, bm, cm = jnp.split(xbc, [SSM_INNER, SSM_INNER + SSM_GROUPS * SSM_STATE], axis=-1)
    xs = xs.reshape(b, l, SSM_HEADS, SSM_HEAD_DIM).astype(jnp.float32)
    bm = bm.reshape(b, l, SSM_GROUPS, SSM_STATE).astype(jnp.float32)
    cm = cm.reshape(b, l, SSM_GROUPS, SSM_STATE).astype(jnp.float32)
    dt = jax.nn.softplus(dt_raw.astype(jnp.float32) + dt_bias.astype(jnp.float32))
    a = -jnp.exp(a_log.astype(jnp.float32))
    chunk = SSM_CHUNK if l % SSM_CHUNK == 0 else l
    y, ssm_new = ssd_chunked(xs * dt[..., None], dt * a, bm, cm, ssm0.astype(jnp.float32), chunk)
    y = y + xs * d_skip.astype(jnp.float32)[:, None]
    y = gated_group_rms(y.reshape(b, l, SSM_INNER), z, norm_g).astype(h.dtype)
    u = glu_a * jax.nn.sigmoid(glu_b)
    u_hist = jnp.concatenate([convc0.astype(h.dtype), u], axis=1)
    new_convc = u_hist[:, -(CONF_WIDTH - 1):]
    v = jax.nn.silu(layer_norm(causal_dwconv(u_hist, dw_w, dw_b), ln_g, ln_b))
    out = jnp.concatenate([y, v], axis=-1) @ w_out
    return out, ssm_new, new_convx, new_convc


def sb_attend(q, k, v, bias, q_offset):
    b, lq, nh, dh = q.shape
    lk = k.shape[1]
    blk = Q_BLOCK if lq % Q_BLOCK == 0 else lq
    nb = lq // blk
    qb = q.reshape(b, nb, blk, nh, dh).transpose(1, 0, 2, 3, 4)
    key_pos = jnp.arange(lk)
    scale = dh ** -0.5
    bias_f = bias.astype(jnp.float32)[None, :, None, None]

    def one_block(args):
        q_blk, i = args
        q_pos = q_offset + i * blk + jnp.arange(blk)
        z = jnp.einsum('bqhd,bshd->bhqs', q_blk, k).astype(jnp.float32) * scale + bias_f
        m = key_pos[None, :] < q_pos[:, None]
        log_keep = jnp.where(m, jax.nn.log_sigmoid(-z), 0.0)
        suffix = lax.cumsum(log_keep, axis=3, reverse=True) - log_keep
        w = jnp.where(m, jnp.exp(jax.nn.log_sigmoid(z) + suffix), 0.0)
        return jnp.einsum('bhqs,bshd->bqhd', w, v.astype(jnp.float32))

    o = lax.map(one_block, (qb, jnp.arange(nb)))
    return o.transpose(1, 0, 2, 3, 4).reshape(b, lq, nh, dh)


def sb_mixer(h, k_past, v_past, w_qkv, q_g, k_g, sb_b, w_o):
    b, l, _ = h.shape
    qkv = (h @ w_qkv).reshape(b, l, 3, SB_HEADS, SB_HEAD_DIM)
    q = rms_norm(qkv[:, :, 0], q_g)
    k = rms_norm(qkv[:, :, 1], k_g)
    v = qkv[:, :, 2]
    if k_past is None:
        k_all, v_all, offset = k, v, 0
    else:
        k_all = jnp.concatenate([k_past.astype(k.dtype), k], axis=1)
        v_all = jnp.concatenate([v_past.astype(v.dtype), v], axis=1)
        offset = k_past.shape[1]
    o = sb_attend(q, k_all, v_all, sb_b, offset).astype(h.dtype)
    return o.reshape(b, l, D_MODEL) @ w_o, k, v


def moe_ffn(h, w_r, b_r, w1, b1, w2, b2):
    shp = h.shape
    xt = h.reshape(-1, shp[-1])
    n = xt.shape[0]
    logits = (xt @ w_r).astype(jnp.float32) + b_r.astype(jnp.float32)
    top_val, top_idx = lax.top_k(logits, TOP_K)
    gates = jax.nn.softmax(top_val, axis=-1)
    flat_e = top_idx.reshape(-1)
    n_asg = n * TOP_K
    order = jnp.argsort(flat_e)
    e_sorted = flat_e[order]
    tok_sorted = (order // TOP_K).astype(jnp.int32)
    gate_sorted = gates.reshape(-1)[order]
    counts = jnp.bincount(flat_e, length=N_EXPERTS)
    padded = (counts + MOE_BLOCK - 1) // MOE_BLOCK * MOE_BLOCK
    pad_end = jnp.cumsum(padded)
    pad_start = pad_end - padded
    start = jnp.cumsum(counts) - counts
    dest = pad_start[e_sorted] + jnp.arange(n_asg) - start[e_sorted]
    n_blocks = -(-n_asg // MOE_BLOCK) + N_EXPERTS
    rows = n_blocks * MOE_BLOCK
    row_tok = jnp.zeros((rows,), jnp.int32).at[dest].set(tok_sorted)
    row_gate = jnp.zeros((rows,), jnp.float32).at[dest].set(gate_sorted)
    block_e = jnp.minimum(jnp.searchsorted(pad_end, jnp.arange(n_blocks) * MOE_BLOCK, side='right'),
                          N_EXPERTS - 1)
    xb = xt[row_tok].reshape(n_blocks, MOE_BLOCK, shp[-1])

    def expert_block(args):
        x_blk, e = args
        gu = x_blk @ w1[e] + b1[e]
        x_glu, x_lin = jnp.split(gu, 2, axis=-1)
        x_glu = jnp.minimum(x_glu, SWIGLU_LIMIT)
        x_lin = jnp.clip(x_lin, -SWIGLU_LIMIT, SWIGLU_LIMIT)
        act = x_glu * jax.nn.sigmoid(SWIGLU_ALPHA * x_glu) * (x_lin + 1.0)
        return act @ w2[e] + b2[e]

    yb = lax.map(expert_block, (xb, block_e)).reshape(rows, shp[-1])
    out = jax.ops.segment_sum(yb * row_gate[:, None].astype(yb.dtype), row_tok, num_segments=n)
    return out.reshape(shp).astype(h.dtype)


def run_group(x, c, ssm0, convx0, convc0, cache_k, cache_v, page_table, w):
    (w_in_ab, conv_xbc_w, conv_xbc_b, dt_bias, a_log, d_skip, ssm_norm_g, conf_dw_w, conf_dw_b,
     conf_ln_g, conf_ln_b, w_out_ab, w_qkv, q_norm_g, k_norm_g, sb_bias, w_o, ada_w, ada_b, norm_mix_g,
     norm_ffn_g, router_w, router_b, exp_w1, exp_b1, exp_w2, exp_b2) = w
    bsz = x.shape[0]
    c_act = jax.nn.silu(c)
    new_ssm, new_cx, new_cc, new_k, new_v = [], [], [], [], []
    for layer in range(DEPTH):
        mod = (c_act @ ada_w[layer] + ada_b[layer])[:, None, :]
        sh_m, sc_m, g_m, sh_f, sc_f, g_f = jnp.split(mod, 6, axis=-1)
        h = rms_norm(x, norm_mix_g[layer]) * (1.0 + sc_m) + sh_m
        j = layer // 2
        if layer % 2 == 0:
            out, s_new, cx_new, cc_new = ab_mixer(
                h, ssm0[j], convx0[j], convc0[j], w_in_ab[j], conv_xbc_w[j], conv_xbc_b[j],
                dt_bias[j], a_log[j], d_skip[j], ssm_norm_g[j], conf_dw_w[j], conf_dw_b[j],
                conf_ln_g[j], conf_ln_b[j], w_out_ab[j])
            new_ssm.append(s_new)
            new_cx.append(cx_new)
            new_cc.append(cc_new)
        else:
            if cache_k is None:
                k_past, v_past = None, None
            else:
                k_past = cache_k[j][page_table].reshape(bsz, -1, SB_HEADS, SB_HEAD_DIM)
                v_past = cache_v[j][page_table].reshape(bsz, -1, SB_HEADS, SB_HEAD_DIM)
            out, k_new, v_new = sb_mixer(h, k_past, v_past, w_qkv[j], q_norm_g[j], k_norm_g[j],
                                         sb_bias[j], w_o[j])
            new_k.append(k_new)
            new_v.append(v_new)
        x = x + g_m * out
        h = rms_norm(x, norm_ffn_g[layer]) * (1.0 + sc_f) + sh_f
        x = x + g_f * moe_ffn(h, router_w[layer], router_b[layer], exp_w1[layer], exp_b1[layer],
                              exp_w2[layer], exp_b2[layer])
    return (x, jnp.stack(new_ssm), jnp.stack(new_cx), jnp.stack(new_cc), jnp.stack(new_k), jnp.stack(new_v))


def setup_inputs(seed: int = 0) -> dict:
    key = jax.random.key(seed)
    ks = list(jax.random.split(key, 64))
    f32 = jnp.float32

    def nrm(shape, scale):
        return jax.random.normal(ks.pop(), shape, f32) * scale

    n_pages = PAST_LEN // PAGE_SIZE
    n_used = DEC_BATCH * n_pages
    n_pool = n_used + (n_used + 3) // 4
    page_table = jax.random.permutation(ks.pop(), n_pool)[:n_used].reshape(DEC_BATCH, n_pages).astype(jnp.int32)
    dt0 = jnp.exp(jax.random.uniform(ks.pop(), (N_AB, SSM_HEADS), f32, math.log(1e-3), math.log(1e-1)))
    dt_bias = dt0 + jnp.log(-jnp.expm1(-dt0))
    a_log = jnp.log(jax.random.uniform(ks.pop(), (N_AB, SSM_HEADS), f32, 1.0, 16.0))
    return {
        'x_prompt': nrm((BATCH, SEQ, D_MODEL), 1.0),
        'x_sample': nrm((DEC_BATCH, DEC_SEQ, D_MODEL), 1.0),
        'c_prompt': nrm((BATCH, D_MODEL), 1.0),
        'c_sample': nrm((DEC_BATCH, D_MODEL), 1.0),
        'state_ssm': nrm((N_AB, DEC_BATCH, SSM_HEADS, SSM_HEAD_DIM, SSM_STATE), 0.1),
        'state_conv_xbc': nrm((N_AB, DEC_BATCH, SSM_CONV - 1, XBC_DIM), 1.0),
        'state_conv_conf': nrm((N_AB, DEC_BATCH, CONF_WIDTH - 1, CONF_CH), 0.5),
        'cache_k': nrm((N_SB, n_pool, PAGE_SIZE, SB_HEADS, SB_HEAD_DIM), 1.0),
        'cache_v': nrm((N_SB, n_pool, PAGE_SIZE, SB_HEADS, SB_HEAD_DIM), 1.0),
        'page_table': page_table,
        'w_in_ab': nrm((N_AB, D_MODEL, IN_AB), D_MODEL ** -0.5),
        'conv_xbc_w': nrm((N_AB, SSM_CONV, XBC_DIM), SSM_CONV ** -0.5),
        'conv_xbc_b': nrm((N_AB, XBC_DIM), 0.02),
        'dt_bias': dt_bias,
        'a_log': a_log,
        'd_skip': 1.0 + nrm((N_AB, SSM_HEADS), 0.1),
        'ssm_norm_g': 1.0 + nrm((N_AB, SSM_INNER), 0.02),
        'conf_dw_w': nrm((N_AB, CONF_WIDTH, CONF_CH), CONF_WIDTH ** -0.5),
        'conf_dw_b': nrm((N_AB, CONF_CH), 0.02),
        'conf_ln_g': 1.0 + nrm((N_AB, CONF_CH), 0.02),
        'conf_ln_b': nrm((N_AB, CONF_CH), 0.02),
        'w_out_ab': nrm((N_AB, MIX_AB, D_MODEL), MIX_AB ** -0.5),
        'w_qkv': nrm((N_SB, D_MODEL, 3 * D_MODEL), D_MODEL ** -0.5),
        'q_norm_g': 1.0 + nrm((N_SB, SB_HEAD_DIM), 0.02),
        'k_norm_g': 1.0 + nrm((N_SB, SB_HEAD_DIM), 0.02),
        'sb_bias': SB_BIAS_INIT + nrm((N_SB, SB_HEADS), 0.1),
        'w_o': nrm((N_SB, D_MODEL, D_MODEL), D_MODEL ** -0.5),
        'ada_w': nrm((DEPTH, D_MODEL, 6 * D_MODEL), 0.5 * D_MODEL ** -0.5),
        'ada_b': nrm((DEPTH, 6 * D_MODEL), 0.02),
        'norm_mix_g': 1.0 + nrm((DEPTH, D_MODEL), 0.02),
        'norm_ffn_g': 1.0 + nrm((DEPTH, D_MODEL), 0.02),
        'router_w': nrm((DEPTH, D_MODEL, N_EXPERTS), D_MODEL ** -0.5),
        'router_b': nrm((DEPTH, N_EXPERTS), 0.01),
        'exp_w1': nrm((DEPTH, N_EXPERTS, D_MODEL, 2 * D_FF), D_MODEL ** -0.5),
        'exp_b1': nrm((DEPTH, N_EXPERTS, 2 * D_FF), 0.02),
        'exp_w2': nrm((DEPTH, N_EXPERTS, D_FF, D_MODEL), D_FF ** -0.5),
        'exp_b2': nrm((DEPTH, N_EXPERTS, D_MODEL), 0.02),
    }


def reference(x_prompt, x_sample, c_prompt, c_sample, state_ssm, state_conv_xbc, state_conv_conf,
              cache_k, cache_v, page_table, w_in_ab, conv_xbc_w, conv_xbc_b, dt_bias, a_log, d_skip,
              ssm_norm_g, conf_dw_w, conf_dw_b, conf_ln_g, conf_ln_b, w_out_ab, w_qkv, q_norm_g,
              k_norm_g, sb_bias, w_o, ada_w, ada_b, norm_mix_g, norm_ffn_g, router_w, router_b, exp_w1,
              exp_b1, exp_w2, exp_b2):
    weights = (w_in_ab, conv_xbc_w, conv_xbc_b, dt_bias, a_log, d_skip, ssm_norm_g, conf_dw_w,
               conf_dw_b, conf_ln_g, conf_ln_b, w_out_ab, w_qkv, q_norm_g, k_norm_g, sb_bias, w_o, ada_w,
               ada_b, norm_mix_g, norm_ffn_g, router_w, router_b, exp_w1, exp_b1, exp_w2, exp_b2)
    ssm0 = jnp.zeros((N_AB, BATCH, SSM_HEADS, SSM_HEAD_DIM, SSM_STATE), jnp.float32)
    cx0 = jnp.zeros((N_AB, BATCH, SSM_CONV - 1, XBC_DIM), x_prompt.dtype)
    cc0 = jnp.zeros((N_AB, BATCH, CONF_WIDTH - 1, CONF_CH), x_prompt.dtype)
    y_prompt, ssm_p, cx_p, cc_p, k_p, v_p = run_group(
        x_prompt, c_prompt, ssm0, cx0, cc0, None, None, None, weights)
    y_sample, ssm_s, cx_s, cc_s, k_s, v_s = run_group(
        x_sample, c_sample, state_ssm, state_conv_xbc, state_conv_conf, cache_k, cache_v, page_table, weights)
    return (y_prompt, y_sample, ssm_p, cx_p, cc_p, k_p, v_p, ssm_s, cx_s, cc_s, k_s, v_s)
```

```python
import functools

import jax
import jax.numpy as jnp
from jax import lax
from jax.experimental import pallas as pl
from jax.experimental.pallas import tpu as pltpu

F32 = jnp.float32
BF16 = jnp.bfloat16
HI = lax.Precision.HIGHEST

D_MODEL = 1024
SSM_HEADS = 16
SSM_HEAD_DIM = 64
SSM_INNER = SSM_HEADS * SSM_HEAD_DIM
SSM_GROUPS = 2
SSM_STATE = 128
SSM_CONV = 4
SSM_CHUNK = 128
XBC_DIM = SSM_INNER + 2 * SSM_GROUPS * SSM_STATE
CONF_CH = 1024
CONF_WIDTH = 31
SB_HEADS = 16
SB_HEAD_DIM = D_MODEL // SB_HEADS
N_EXPERTS = 32
TOP_K = 4
D_FF = D_MODEL
SWIGLU_ALPHA = 1.702
SWIGLU_LIMIT = 7.0
EPS = 1e-6
PAGE_SIZE = 128

LANES = 128
SUBLANES = 8
VMEM_LIMIT_BYTES = 56 * 1024 * 1024

PROJ_Z, PROJ_GA, PROJ_GB, PROJ_XBC = 0, 1, 2, 2
PROJ_W = 3 * 1024 + XBC_DIM


def _cparams(sem):
    return pltpu.CompilerParams(dimension_semantics=sem, vmem_limit_bytes=VMEM_LIMIT_BYTES)


def _softplus(x):
    return jnp.maximum(x, 0.0) + jnp.log(1.0 + jnp.exp(-jnp.abs(x)))


def _sigmoid(x):
    return 1.0 / (1.0 + jnp.exp(-x))


def _split_bf16(x):
    hi = x.astype(BF16)
    lo = (x - hi.astype(F32)).astype(BF16)
    return hi, lo


def _norm_mod(x, g, sc, sh):
    ms = jnp.mean(x * x, axis=-1, keepdims=True)
    return (x * lax.rsqrt(ms + EPS) * g) * (1.0 + sc) + sh


def _mod_spec(per_row, tm, rows_per_batch, chunk):
    if per_row:
        return pl.BlockSpec((tm, D_MODEL), lambda i, *_: (i, chunk))
    return pl.BlockSpec((None, 1, D_MODEL), lambda i, *_: ((i * tm) // rows_per_batch, 0, chunk))


def _ada_kernel(c_ref, w_ref, b_ref, o_ref):
    c = c_ref[...]
    c_act = c * _sigmoid(c)
    o_ref[...] = jnp.dot(c_act, w_ref[...], precision=HI, preferred_element_type=F32) + b_ref[...]


def _ada_all(c_pad, ada_w, ada_b):
    depth = ada_w.shape[0]
    bp = c_pad.shape[0]
    n_chunks = ada_w.shape[2] // D_MODEL
    return pl.pallas_call(
        _ada_kernel,
        grid=(depth, n_chunks),
        in_specs=[pl.BlockSpec((bp, D_MODEL), lambda l, j: (0, 0)),
                  pl.BlockSpec((None, D_MODEL, D_MODEL), lambda l, j: (l, 0, j)),
                  pl.BlockSpec((None, 1, D_MODEL), lambda l, j: (l, 0, j))],
        out_specs=pl.BlockSpec((None, bp, D_MODEL), lambda l, j: (l, 0, j)),
        out_shape=jax.ShapeDtypeStruct((depth, bp, ada_w.shape[2]), F32),
        compiler_params=_cparams(("arbitrary", "arbitrary")),
        name="adaln_mod",
    )(c_pad, ada_w, ada_b.reshape(depth, 1, -1))


def _nmm_kernel(x_ref, g_ref, sc_ref, sh_ref, w_ref, o_ref, h_ref):
    @pl.when(pl.program_id(1) == 0)
    def _():
        h_ref[...] = _norm_mod(x_ref[...], g_ref[...], sc_ref[...], sh_ref[...]).astype(BF16)

    o_ref[...] = jnp.dot(h_ref[...], w_ref[...], preferred_element_type=F32)


def _norm_mod_matmul(x, g, mod, per_row, rows_per_batch, chunk_sc, chunk_sh, w_bf16, tn):
    n, d = x.shape
    nout = w_bf16.shape[1]
    tm = min(512, n)
    return pl.pallas_call(
        _nmm_kernel,
        grid=(n // tm, nout // tn),
        in_specs=[pl.BlockSpec((tm, d), lambda i, j: (i, 0)),
                  pl.BlockSpec((1, d), lambda i, j: (0, 0)),
                  _mod_spec(per_row, tm, rows_per_batch, chunk_sc),
                  _mod_spec(per_row, tm, rows_per_batch, chunk_sh),
                  pl.BlockSpec((d, tn), lambda i, j: (0, j))],
        out_specs=pl.BlockSpec((tm, tn), lambda i, j: (i, j)),
        out_shape=jax.ShapeDtypeStruct((n, nout), F32),
        scratch_shapes=[pltpu.VMEM((tm, d), BF16)],
        compiler_params=_cparams(("arbitrary", "arbitrary")),
        name="norm_mod_matmul",
    )(x, g.reshape(1, d), mod, mod, w_bf16)


QKV_TN = 512


def _qkv_kernel(x_ref, g_ref, sc_ref, sh_ref, w_ref, p_ref, qg_ref, kg_ref,
                q_ref, k_ref, kb_ref, v_ref, vb_ref, h_ref):
    j = pl.program_id(1)
    n_q = D_MODEL // QKV_TN

    @pl.when(j == 0)
    def _():
        h_ref[...] = _norm_mod(x_ref[...], g_ref[...], sc_ref[...], sh_ref[...]).astype(BF16)

    acc = jnp.dot(h_ref[...], w_ref[...], preferred_element_type=F32)

    def head_norm(a, gain):
        hi, lo = _split_bf16(a * a)
        ss = (jnp.dot(hi, p_ref[...], preferred_element_type=F32)
              + jnp.dot(lo, p_ref[...], preferred_element_type=F32))
        return a * lax.rsqrt(ss * (1.0 / SB_HEAD_DIM) + EPS) * gain

    @pl.when(j < n_q)
    def _():
        q_ref[...] = (head_norm(acc, qg_ref[...]) * (SB_HEAD_DIM ** -0.5)).astype(BF16)

    @pl.when((j >= n_q) & (j < 2 * n_q))
    def _():
        kn = head_norm(acc, kg_ref[...])
        k_ref[...] = kn
        kb_ref[...] = kn.astype(BF16)

    @pl.when(j >= 2 * n_q)
    def _():
        v_ref[...] = acc
        vb_ref[...] = acc.astype(BF16)


def _qkv_proj(x, g, mod, per_row, rows_per_batch, w_bf16, q_gain, k_gain):
    n, d = x.shape
    tm = min(512, n)
    tn = QKV_TN
    n_q = d // tn
    heads_per_tile = tn // SB_HEAD_DIM
    hid = jnp.arange(tn) // SB_HEAD_DIM
    pmat = (hid[:, None] == hid[None, :]).astype(BF16)
    qg = jnp.tile(q_gain.astype(F32), heads_per_tile).reshape(1, tn)
    kg = jnp.tile(k_gain.astype(F32), heads_per_tile).reshape(1, tn)

    def out_map(lo):
        return lambda i, j: (i, jnp.clip(j - lo, 0, n_q - 1))

    shp = lambda dt: jax.ShapeDtypeStruct((n, d), dt)
    return pl.pallas_call(
        _qkv_kernel,
        grid=(n // tm, 3 * n_q),
        in_specs=[pl.BlockSpec((tm, d), lambda i, j: (i, 0)),
                  pl.BlockSpec((1, d), lambda i, j: (0, 0)),
                  _mod_spec(per_row, tm, rows_per_batch, 1),
                  _mod_spec(per_row, tm, rows_per_batch, 0),
                  pl.BlockSpec((d, tn), lambda i, j: (0, j)),
                  pl.BlockSpec((tn, tn), lambda i, j: (0, 0)),
                  pl.BlockSpec((1, tn), lambda i, j: (0, 0)),
                  pl.BlockSpec((1, tn), lambda i, j: (0, 0))],
        out_specs=[pl.BlockSpec((tm, tn), out_map(0)),
                   pl.BlockSpec((tm, tn), out_map(n_q)),
                   pl.BlockSpec((tm, tn), out_map(n_q)),
                   pl.BlockSpec((tm, tn), out_map(2 * n_q)),
                   pl.BlockSpec((tm, tn), out_map(2 * n_q))],
        out_shape=[shp(BF16), shp(F32), shp(BF16), shp(F32), shp(BF16)],
        scratch_shapes=[pltpu.VMEM((tm, d), BF16)],
        compiler_params=_cparams(("arbitrary", "arbitrary")),
        name="qkv_proj",
    )(x, g.reshape(1, d), mod, mod, w_bf16, pmat, qg, kg)


def _mmres_kernel(*refs, n_in):
    a_refs = refs[:n_in]
    w_ref = refs[n_in]
    x_ref, g_ref, o_ref = refs[n_in + 1:]
    acc = jnp.dot(a_refs[0][...], w_ref[0], preferred_element_type=F32)
    for i in range(1, n_in):
        acc = acc + jnp.dot(a_refs[i][...], w_ref[i], preferred_element_type=F32)
    o_ref[...] = x_ref[...] + g_ref[...] * acc


def _matmul_residual(a_list, w_bf16, x, mod, per_row, rows_per_batch, chunk_gate):
    n, d = x.shape
    n_in = len(a_list)
    tm = min(512, n)
    w3 = w_bf16.reshape(n_in, d, d)
    return pl.pallas_call(
        functools.partial(_mmres_kernel, n_in=n_in),
        grid=(n // tm,),
        in_specs=([pl.BlockSpec((tm, d), lambda i: (i, 0)) for _ in range(n_in)]
                  + [pl.BlockSpec((n_in, d, d), lambda i: (0, 0, 0)),
                     pl.BlockSpec((tm, d), lambda i: (i, 0)),
                     _mod_spec(per_row, tm, rows_per_batch, chunk_gate)]),
        out_specs=pl.BlockSpec((tm, d), lambda i: (i, 0)),
        out_shape=jax.ShapeDtypeStruct((n, d), F32),
        compiler_params=_cparams(("arbitrary",)),
        name="matmul_residual",
    )(*a_list, w3, x, mod)


SSD_T = SSM_CHUNK
HIST0 = SUBLANES
N_PAIRS = SSM_HEADS // 2
PAIRS_PER_GROUP = N_PAIRS // SSM_GROUPS


def _ssd_kernel(xbc_ref, z_ref, dt_ref, s0_ref, c0_ref, cw_ref, cb_ref, dtb_ref, alog_ref, dexp_ref,
                ng_ref, expand_ref, y_ref, s_ref, cn_ref, hist_ref, dts_ref, zs_ref, y32_ref, *, tin):
    T = SSD_T
    kc = SSM_CONV
    c = pl.program_id(1)

    @pl.when(c == 0)
    def _init():
        s_ref[...] = s0_ref[...]
        if tin < T:
            hist_ref[...] = jnp.zeros_like(hist_ref)
            dts_ref[...] = jnp.zeros_like(dts_ref)
            zs_ref[...] = jnp.zeros_like(zs_ref)
        hist_ref[HIST0 - (kc - 1):HIST0, :] = c0_ref[...]

    hist_ref[HIST0:HIST0 + tin, :] = xbc_ref[...]
    new_tail = hist_ref[HIST0 + tin - (kc - 1):HIST0 + tin, :]
    cn_ref[...] = new_tail

    acc = cb_ref[...] + cw_ref[0:1, :] * hist_ref[HIST0 - (kc - 1):HIST0 - (kc - 1) + T, :]
    for k in range(1, kc):
        acc = acc + cw_ref[k:k + 1, :] * hist_ref[HIST0 - (kc - 1) + k:HIST0 - (kc - 1) + k + T, :]
    hist_ref[HIST0 - (kc - 1):HIST0, :] = new_tail
    act = acc * _sigmoid(acc)
    xs = act[:, :SSM_INNER]
    bm = [act[:, SSM_INNER + g * SSM_STATE:SSM_INNER + (g + 1) * SSM_STATE].astype(BF16)
          for g in range(SSM_GROUPS)]
    cm = [act[:, SSM_INNER + (SSM_GROUPS + g) * SSM_STATE:SSM_INNER + (SSM_GROUPS + g + 1) * SSM_STATE].astype(BF16)
          for g in range(SSM_GROUPS)]

    if tin < T:
        dts_ref[0:tin, :] = dt_ref[...]
        zs_ref[0:tin, :] = z_ref[...]
        dt_raw = dts_ref[...]
        z = zs_ref[...]
    else:
        dt_raw = dt_ref[...]
        z = z_ref[...]
    row = lax.broadcasted_iota(jnp.int32, (T, T), 0)
    col = lax.broadcasted_iota(jnp.int32, (T, T), 1)
    tril = row >= col
    dt = _softplus(dt_raw + dtb_ref[...])
    if tin < T:
        dt = jnp.where(lax.broadcasted_iota(jnp.int32, dt.shape, 0) < tin, dt, 0.0)
    a = dt * (-jnp.exp(alog_ref[...]))
    a_cs = jnp.dot(tril.astype(F32), a, precision=HI, preferred_element_type=F32)
    a_cs_t = a_cs.T
    expand = expand_ref[...]
    a_cs_x = jnp.dot(a_cs, expand, precision=HI, preferred_element_type=F32)
    dt_x = jnp.dot(dt, expand, precision=HI, preferred_element_type=F32)
    xdt = xs * dt_x
    e_acs = jnp.exp(a_cs_x)
    xds = xdt * jnp.exp(a_cs_x[T - 1:T, :] - a_cs_x)
    nt = (((1,), (1,)), ((), ()))
    cbm = [lax.dot_general(cm[g], bm[g], nt, preferred_element_type=F32) for g in range(SSM_GROUPS)]
    lane = lax.broadcasted_iota(jnp.int32, (T, LANES), 1)
    sub = lax.broadcasted_iota(jnp.int32, (LANES, 1), 0)
    neg_inf = jnp.float32(-jnp.inf)
    for p in range(N_PAIRS):
        g = p // PAIRS_PER_GROUP
        sl = slice(p * LANES, (p + 1) * LANES)
        xp = xdt[:, sl].astype(BF16)
        yd = []
        for h in (2 * p, 2 * p + 1):
            seg = jnp.where(tril, a_cs[:, h:h + 1] - a_cs_t[h:h + 1, :], neg_inf)
            m = (cbm[g] * jnp.exp(seg)).astype(BF16)
            yd.append(jnp.dot(m, xp, preferred_element_type=F32))
        y_diag = jnp.where(lane < SSM_HEAD_DIM, yd[0], yd[1])
        s_pair = s_ref[p]
        y_off = lax.dot_general(cm[g], s_pair.astype(BF16), nt, preferred_element_type=F32) * e_acs[:, sl]
        y32_ref[:, sl] = y_diag + y_off + xs[:, sl] * dexp_ref[:, sl]
        upd = jnp.dot(xds[:, sl].T.astype(BF16), bm[g], preferred_element_type=F32)
        decay = jnp.where(sub < SSM_HEAD_DIM,
                          jnp.exp(a_cs_t[2 * p:2 * p + 1, T - 1:T]), jnp.exp(a_cs_t[2 * p + 1:2 * p + 2, T - 1:T]))
        s_ref[p] = decay * s_pair + upd

    yf = y32_ref[...] * (z * _sigmoid(z))
    gw = SSM_INNER // SSM_GROUPS
    outs = []
    for g in range(SSM_GROUPS):
        yg = yf[:, g * gw:(g + 1) * gw]
        outs.append(yg * lax.rsqrt(jnp.mean(yg * yg, axis=-1, keepdims=True) + EPS))
    out = jnp.concatenate(outs, axis=-1) * ng_ref[...]
    y_ref[...] = out[0:tin, :].astype(BF16)


def _ssd(proj3, dt3, ssm0, convx0, conv_w, conv_b, dt_bias, a_log, d_skip, norm_g):
    b, l, _ = proj3.shape
    tin = SSD_T if l % SSD_T == 0 else l
    nc = l // tin
    pad = LANES - SSM_HEADS
    dtb = jnp.pad(dt_bias.astype(F32), (0, pad)).reshape(1, LANES)
    alog = jnp.pad(a_log.astype(F32), (0, pad)).reshape(1, LANES)
    dexp = jnp.repeat(d_skip.astype(F32), SSM_HEAD_DIM).reshape(1, SSM_INNER)
    expand = (jnp.arange(LANES)[:, None] == (jnp.arange(SSM_INNER) // SSM_HEAD_DIM)[None, :]).astype(F32)
    s0 = ssm0.astype(F32).reshape(b, N_PAIRS, 2 * SSM_HEAD_DIM, SSM_STATE)
    kern = functools.partial(_ssd_kernel, tin=tin)
    const = lambda shape: pl.BlockSpec(shape, lambda i, c: tuple(0 for _ in shape))
    y, s_new, c_new = pl.pallas_call(
        kern,
        grid=(b, nc),
        in_specs=[pl.BlockSpec((None, tin, XBC_DIM), lambda i, c: (i, c, PROJ_XBC)),
                  pl.BlockSpec((None, tin, SSM_INNER), lambda i, c: (i, c, PROJ_Z)),
                  pl.BlockSpec((None, tin, LANES), lambda i, c: (i, c, 0)),
                  pl.BlockSpec((None, N_PAIRS, 2 * SSM_HEAD_DIM, SSM_STATE), lambda i, c: (i, 0, 0, 0)),
                  pl.BlockSpec((None, SSM_CONV - 1, XBC_DIM), lambda i, c: (i, 0, 0)),
                  const((SSM_CONV, XBC_DIM)), const((1, XBC_DIM)), const((1, LANES)), const((1, LANES)),
                  const((1, SSM_INNER)), const((1, SSM_INNER)), const((LANES, SSM_INNER))],
        out_specs=[pl.BlockSpec((None, tin, SSM_INNER), lambda i, c: (i, c, 0)),
                   pl.BlockSpec((None, N_PAIRS, 2 * SSM_HEAD_DIM, SSM_STATE), lambda i, c: (i, 0, 0, 0)),
                   pl.BlockSpec((None, SSM_CONV - 1, XBC_DIM), lambda i, c: (i, 0, 0))],
        out_shape=[jax.ShapeDtypeStruct((b, l, SSM_INNER), BF16),
                   jax.ShapeDtypeStruct((b, N_PAIRS, 2 * SSM_HEAD_DIM, SSM_STATE), F32),
                   jax.ShapeDtypeStruct((b, SSM_CONV - 1, XBC_DIM), F32)],
        scratch_shapes=[pltpu.VMEM((HIST0 + SSD_T, XBC_DIM), F32),
                        pltpu.VMEM((SSD_T, LANES), F32),
                        pltpu.VMEM((SSD_T, SSM_INNER), F32),
                        pltpu.VMEM((SSD_T, SSM_INNER), F32)],
        compiler_params=_cparams(("arbitrary", "arbitrary")),
        name="ssd_scan",
    )(proj3, proj3, dt3, s0, convx0.astype(F32), conv_w.astype(F32), conv_b.astype(F32).reshape(1, -1),
      dtb, alog, dexp, norm_g.astype(F32).reshape(1, -1), expand)
    return y, s_new.reshape(b, SSM_HEADS, SSM_HEAD_DIM, SSM_STATE), c_new


CONF_T = 256
CONF_HIST0 = 32
CONF_RC = 128


def _conf_kernel(ga_ref, gb_ref, c0_ref, w_ref, b_ref, lg_ref, lb_ref, v_ref, cn_ref, hist_ref, conv_ref, *, tin, T):
    kw = CONF_WIDTH
    h0 = CONF_HIST0 - (kw - 1)
    c = pl.program_id(1)

    @pl.when(c == 0)
    def _init():
        if tin < T:
            hist_ref[...] = jnp.zeros_like(hist_ref)
        hist_ref[h0:CONF_HIST0, :] = c0_ref[...]

    hist_ref[CONF_HIST0:CONF_HIST0 + tin, :] = ga_ref[...] * _sigmoid(gb_ref[...])
    new_tail = hist_ref[h0 + tin:CONF_HIST0 + tin, :]
    cn_ref[...] = new_tail

    rc = min(CONF_RC, T)
    for r in range(0, T, rc):
        for cc in range(0, CONF_CH, LANES):
            acc = b_ref[:, cc:cc + LANES] + w_ref[0:1, cc:cc + LANES] * hist_ref[h0 + r:h0 + r + rc, cc:cc + LANES]
            for k in range(1, kw):
                acc = acc + w_ref[k:k + 1, cc:cc + LANES] * hist_ref[h0 + r + k:h0 + r + k + rc, cc:cc + LANES]
            conv_ref[r:r + rc, cc:cc + LANES] = acc
    hist_ref[h0:CONF_HIST0, :] = new_tail

    y = conv_ref[...]
    mu = jnp.mean(y, axis=-1, keepdims=True)
    yc = y - mu
    var = jnp.mean(yc * yc, axis=-1, keepdims=True)
    yn = yc * lax.rsqrt(var + EPS) * lg_ref[...] + lb_ref[...]
    v = yn * _sigmoid(yn)
    v_ref[...] = v[0:tin, :].astype(BF16)


def _conformer(proj3, convc0, dw_w, dw_b, ln_g, ln_b):
    b, l, _ = proj3.shape
    if l % CONF_T == 0:
        tin, T = CONF_T, CONF_T
    else:
        tin, T = l, -(-l // SUBLANES) * SUBLANES
    nc = l // tin
    kern = functools.partial(_conf_kernel, tin=tin, T=T)
    const = lambda shape: pl.BlockSpec(shape, lambda i, c: tuple(0 for _ in shape))
    v, c_new = pl.pallas_call(
        kern,
        grid=(b, nc),
        in_specs=[pl.BlockSpec((None, tin, CONF_CH), lambda i, c: (i, c, PROJ_GA)),
                  pl.BlockSpec((None, tin, CONF_CH), lambda i, c: (i, c, PROJ_GB)),
                  pl.BlockSpec((None, CONF_WIDTH - 1, CONF_CH), lambda i, c: (i, 0, 0)),
                  const((CONF_WIDTH, CONF_CH)), const((1, CONF_CH)), const((1, CONF_CH)), const((1, CONF_CH))],
        out_specs=[pl.BlockSpec((None, tin, CONF_CH), lambda i, c: (i, c, 0)),
                   pl.BlockSpec((None, CONF_WIDTH - 1, CONF_CH), lambda i, c: (i, 0, 0))],
        out_shape=[jax.ShapeDtypeStruct((b, l, CONF_CH), BF16),
                   jax.ShapeDtypeStruct((b, CONF_WIDTH - 1, CONF_CH), F32)],
        scratch_shapes=[pltpu.VMEM((CONF_HIST0 + T, CONF_CH), F32),
                        pltpu.VMEM((T, CONF_CH), F32)],
        compiler_params=_cparams(("arbitrary", "arbitrary")),
        name="conformer_conv",
    )(proj3, proj3, convc0.astype(F32), dw_w.astype(F32), dw_b.astype(F32).reshape(1, -1),
      ln_g.astype(F32).reshape(1, -1), ln_b.astype(F32).reshape(1, -1))
    return v, c_new


SB_TQ = 256


def _suffix_sum(lk, u_ref):
    hi, lo = _split_bf16(lk)
    return (jnp.dot(hi, u_ref[...], preferred_element_type=F32)
            + jnp.dot(lo, u_ref[...], preferred_element_type=F32))


def _sb_prompt_kernel(bias_ref, q_ref, k_ref, v_ref, u_ref, o_ref, *, tq):
    hp = pl.program_id(1)
    i = pl.program_id(2)
    lane = lax.broadcasted_iota(jnp.int32, (tq, LANES), 1)
    q2 = q_ref[...]
    zero = jnp.zeros_like(q2)
    qs = jnp.concatenate([jnp.where(lane < SB_HEAD_DIM, q2, zero), jnp.where(lane >= SB_HEAD_DIM, q2, zero)], axis=0)
    rows = lax.broadcasted_iota(jnp.int32, (2 * tq, 1), 0)
    bias = jnp.where(rows < tq, bias_ref[2 * hp], bias_ref[2 * hp + 1])
    nt = (((1,), (1,)), ((), ()))

    def block(j, carry, acc, diagonal):
        k2 = k_ref[pl.ds(pl.multiple_of(j * tq, tq), tq), :]
        v2 = v_ref[pl.ds(pl.multiple_of(j * tq, tq), tq), :]
        z = lax.dot_general(qs, k2, nt, preferred_element_type=F32) + bias
        lk = -_softplus(z)
        if diagonal:
            r = lax.broadcasted_iota(jnp.int32, (2 * tq, tq), 0)
            cidx = lax.broadcasted_iota(jnp.int32, (2 * tq, tq), 1)
            valid = cidx < jnp.where(r >= tq, r - tq, r)
            lk = jnp.where(valid, lk, 0.0)
        s = _suffix_sum(lk, u_ref)
        w = jnp.exp(z + s + carry)
        if diagonal:
            w = jnp.where(valid, w, 0.0)
        acc = acc + jnp.dot(w.astype(BF16), v2, preferred_element_type=F32)
        return carry + s[:, 0:1], acc

    carry0 = jnp.zeros((2 * tq, 1), F32)
    acc0 = jnp.zeros((2 * tq, LANES), F32)
    carry, acc = block(i, carry0, acc0, True)

    def body(jj, st):
        return block(i - 1 - jj, st[0], st[1], False)

    carry, acc = lax.fori_loop(0, i, body, (carry, acc))
    o_ref[...] = jnp.where(lane < SB_HEAD_DIM, acc[0:tq, :], acc[tq:2 * tq, :]).astype(BF16)


def _sb_prompt(q, kb, vb, bias):
    b, l, d = q.shape
    tq = min(SB_TQ, l)
    umat = (jnp.arange(tq)[:, None] >= jnp.arange(tq)[None, :]).astype(BF16)
    return pl.pallas_call(
        functools.partial(_sb_prompt_kernel, tq=tq),
        grid_spec=pltpu.PrefetchScalarGridSpec(
            num_scalar_prefetch=0,
            grid=(b, d // LANES, l // tq),
            in_specs=[pl.BlockSpec(memory_space=pltpu.SMEM),
                      pl.BlockSpec((None, tq, LANES), lambda bi, hp, i: (bi, i, hp)),
                      pl.BlockSpec((None, l, LANES), lambda bi, hp, i: (bi, 0, hp)),
                      pl.BlockSpec((None, l, LANES), lambda bi, hp, i: (bi, 0, hp)),
                      pl.BlockSpec((tq, tq), lambda bi, hp, i: (0, 0))],
            out_specs=pl.BlockSpec((None, tq, LANES), lambda bi, hp, i: (bi, i, hp))),
        out_shape=jax.ShapeDtypeStruct((b, l, d), BF16),
        compiler_params=_cparams(("arbitrary", "arbitrary", "arbitrary")),
        name="sb_attention_prompt",
    )(bias.astype(F32), q, kb, vb, umat)


def _sb_decode_kernel(pt_ref, qbd_ref, brow_ref, kn_ref, vn_ref, kc_ref, vc_ref, ut_ref, msk_ref, sel_ref, dmask_ref,
                      o_ref, acc_ref, carry_ref, *, n_new):
    p = pl.program_id(1)
    n_steps = pl.num_programs(1)

    @pl.when(p == 0)
    def _init():
        acc_ref[...] = jnp.zeros_like(acc_ref)
        carry_ref[...] = jnp.zeros_like(carry_ref)

    def page(kp, vp, new_tokens):
        z = jnp.dot(kp.astype(BF16), qbd_ref[...], preferred_element_type=F32) + brow_ref[...]
        lk = -_softplus(z)
        if new_tokens:
            valid = msk_ref[...] > 0.5
            lk = jnp.where(valid, lk, 0.0)
        hi, lo = _split_bf16(lk)
        s = (jnp.dot(ut_ref[...], hi, preferred_element_type=F32)
             + jnp.dot(ut_ref[...], lo, preferred_element_type=F32))
        w = jnp.exp(z + s + carry_ref[...])
        if new_tokens:
            w = jnp.where(valid, w, 0.0)
        acc_ref[...] += jnp.dot(w.T.astype(BF16), vp.astype(BF16), preferred_element_type=F32)
        carry_ref[...] += s[0:1, :]

    @pl.when(p == 0)
    def _new():
        page(kn_ref[...], vn_ref[...], True)

    @pl.when(p > 0)
    def _cached():
        page(kc_ref[...], vc_ref[...], False)

    @pl.when(p == n_steps - 1)
    def _fin():
        masked = acc_ref[...] * dmask_ref[...]
        out = jnp.dot(sel_ref[...], masked, precision=HI, preferred_element_type=F32)
        o_ref[...] = out[0:n_new, :].astype(BF16)


def _sb_decode(q, k_new, v_new, cache_k, cache_v, page_table, bias):
    b, t, d = q.shape
    n_pages = page_table.shape[1]
    n_pool = cache_k.shape[0]
    ck = cache_k.reshape(n_pool, PAGE_SIZE, d)
    cv = cache_v.reshape(n_pool, PAGE_SIZE, d)
    cols = SB_HEADS * t
    assert cols <= LANES
    col_h = jnp.arange(LANES) // t
    col_t = jnp.arange(LANES) % t
    col_ok = jnp.arange(LANES) < cols
    q4 = q.reshape(b, t, SB_HEADS, SB_HEAD_DIM)
    qcols = jnp.transpose(q4, (0, 2, 3, 1))
    eye = jnp.eye(SB_HEADS, dtype=F32)
    qbd = (qcols[:, :, :, None, :] * eye[None, :, None, :, None]).reshape(b, d, cols)
    qbd = jnp.pad(qbd, ((0, 0), (0, 0), (0, LANES - cols))).astype(BF16)
    brow = jnp.where(col_ok, bias.astype(F32)[jnp.minimum(col_h, SB_HEADS - 1)], 0.0).reshape(1, LANES)
    pad_rows = PAGE_SIZE - t
    kn = jnp.pad(k_new, ((0, 0), (0, pad_rows), (0, 0)))
    vn = jnp.pad(v_new, ((0, 0), (0, pad_rows), (0, 0)))
    key = jnp.arange(PAGE_SIZE)
    ut = (key[None, :] >= key[:, None]).astype(BF16)
    msk = ((key[:, None] < col_t[None, :]) & (key[:, None] < t) & col_ok[None, :]).astype(F32)
    sel_rows = -(-t // SUBLANES) * SUBLANES
    sel = ((jnp.arange(sel_rows)[:, None] == col_t[None, :]) & col_ok[None, :]).astype(F32)
    dmask = ((jnp.arange(d) // SB_HEAD_DIM)[None, :] == col_h[:, None]).astype(F32) * col_ok[:, None]
    n_steps = n_pages + 1

    def page_map(bi, p, pt):
        return (pt[bi, n_pages - jnp.maximum(p, 1)], 0, 0)

    const = lambda shape: pl.BlockSpec(shape, lambda bi, p, pt: tuple(0 for _ in shape))
    per_seq = lambda shape: pl.BlockSpec((None,) + shape, lambda bi, p, pt: (bi,) + tuple(0 for _ in shape))
    return pl.pallas_call(
        functools.partial(_sb_decode_kernel, n_new=t),
        grid_spec=pltpu.PrefetchScalarGridSpec(
            num_scalar_prefetch=1,
            grid=(b, n_steps),
            in_specs=[per_seq((d, LANES)), const((1, LANES)),
                      per_seq((PAGE_SIZE, d)), per_seq((PAGE_SIZE, d)),
                      pl.BlockSpec((None, PAGE_SIZE, d), page_map),
                      pl.BlockSpec((None, PAGE_SIZE, d), page_map),
                      const((PAGE_SIZE, PAGE_SIZE)), const((PAGE_SIZE, LANES)),
                      const((sel_rows, LANES)), const((LANES, d))],
            out_specs=per_seq((t, d)),
            scratch_shapes=[pltpu.VMEM((LANES, d), F32), pltpu.VMEM((1, LANES), F32)]),
        out_shape=jax.ShapeDtypeStruct((b, t, d), BF16),
        compiler_params=_cparams(("arbitrary", "arbitrary")),
        name="sb_attention_decode",
    )(page_table.astype(jnp.int32), qbd, brow, kn, vn, ck, cv, ut, msk, sel, dmask)


def _router_kernel(x_ref, g_ref, sc_ref, sh_ref, wrt_ref, br_ref, h_ref, idx_ref, gate_ref):
    h = _norm_mod(x_ref[...], g_ref[...], sc_ref[...], sh_ref[...])
    h_ref[...] = h.astype(BF16)
    nt = (((1,), (1,)), ((), ()))
    logits = lax.dot_general(wrt_ref[...], h, nt, precision=HI, preferred_element_type=F32) + br_ref[...]
    tm = logits.shape[1]
    eidx = lax.broadcasted_iota(jnp.int32, (N_EXPERTS, tm), 0)
    vals = logits
    top_v, top_i = [], []
    for _ in range(TOP_K):
        m = jnp.max(vals, axis=0, keepdims=True)
        idx = jnp.min(jnp.where(vals == m, eidx, N_EXPERTS), axis=0, keepdims=True)
        top_v.append(m)
        top_i.append(idx)
        vals = jnp.where(eidx == idx, -jnp.inf, vals)
    e = [jnp.exp(v - top_v[0]) for v in top_v]
    tot = e[0]
    for k in range(1, TOP_K):
        tot = tot + e[k]
    pad = SUBLANES - TOP_K
    idx_ref[...] = jnp.concatenate(top_i + [jnp.zeros((pad, tm), jnp.int32)], axis=0)
    gate_ref[...] = jnp.concatenate([ek / tot for ek in e] + [jnp.zeros((pad, tm), F32)], axis=0)


def _router(x, g, mod, per_row, rows_per_batch, router_w, router_b):
    n, d = x.shape
    tm = min(512, n)
    return pl.pallas_call(
        _router_kernel,
        grid=(n // tm,),
        in_specs=[pl.BlockSpec((tm, d), lambda i: (i, 0)),
                  pl.BlockSpec((1, d), lambda i: (0, 0)),
                  _mod_spec(per_row, tm, rows_per_batch, 4),
                  _mod_spec(per_row, tm, rows_per_batch, 3),
                  pl.BlockSpec((N_EXPERTS, d), lambda i: (0, 0)),
                  pl.BlockSpec((N_EXPERTS, 1), lambda i: (0, 0))],
        out_specs=[pl.BlockSpec((tm, d), lambda i: (i, 0)),
                   pl.BlockSpec((SUBLANES, tm), lambda i: (0, i)),
                   pl.BlockSpec((SUBLANES, tm), lambda i: (0, i))],
        out_shape=[jax.ShapeDtypeStruct((n, d), BF16),
                   jax.ShapeDtypeStruct((SUBLANES, n), jnp.int32),
                   jax.ShapeDtypeStruct((SUBLANES, n), F32)],
        compiler_params=_cparams(("arbitrary",)),
        name="moe_router",
    )(x, g.reshape(1, d), mod, mod, router_w.astype(F32).T, router_b.astype(F32).reshape(N_EXPERTS, 1))


def _expert_kernel(be_ref, nu_ref, x_ref, w1_ref, b1_ref, w2_ref, b2_ref, o_ref, w1b_ref, w2b_ref):
    blk = pl.program_id(0)
    e = be_ref[blk]
    prev = be_ref[jnp.maximum(blk - 1, 0)]

    @pl.when((blk == 0) | (e != prev))
    def _cast():
        w1b_ref[...] = w1_ref[...].astype(BF16)
        w2b_ref[...] = w2_ref[...].astype(BF16)

    @pl.when(blk < nu_ref[0])
    def _compute():
        gu = jnp.dot(x_ref[...], w1b_ref[...], preferred_element_type=F32) + b1_ref[...]
        x_glu = jnp.minimum(gu[:, :D_FF], SWIGLU_LIMIT)
        x_lin = jnp.clip(gu[:, D_FF:], -SWIGLU_LIMIT, SWIGLU_LIMIT)
        act = x_glu * _sigmoid(SWIGLU_ALPHA * x_glu) * (x_lin + 1.0)
        o_ref[...] = jnp.dot(act.astype(BF16), w2b_ref[...], preferred_element_type=F32) + b2_ref[...]

    @pl.when(blk >= nu_ref[0])
    def _skip():
        o_ref[...] = jnp.zeros_like(o_ref)


def _experts(xb, block_e, n_used, w1, b1, w2, b2, tm):
    rows, d = xb.shape
    n_blocks = rows // tm
    return pl.pallas_call(
        _expert_kernel,
        grid_spec=pltpu.PrefetchScalarGridSpec(
            num_scalar_prefetch=2,
            grid=(n_blocks,),
            in_specs=[pl.BlockSpec((tm, d), lambda i, be, nu: (i, 0)),
                      pl.BlockSpec((None, d, 2 * D_FF), lambda i, be, nu: (be[i], 0, 0)),
                      pl.BlockSpec((None, 1, 2 * D_FF), lambda i, be, nu: (be[i], 0, 0)),
                      pl.BlockSpec((None, D_FF, d), lambda i, be, nu: (be[i], 0, 0)),
                      pl.BlockSpec((None, 1, d), lambda i, be, nu: (be[i], 0, 0))],
            out_specs=pl.BlockSpec((tm, d), lambda i, be, nu: (i, 0)),
            scratch_shapes=[pltpu.VMEM((d, 2 * D_FF), BF16), pltpu.VMEM((D_FF, d), BF16)]),
        out_shape=jax.ShapeDtypeStruct((rows, d), F32),
        compiler_params=_cparams(("arbitrary",)),
        name="moe_experts",
    )(block_e, n_used, xb, w1, b1.reshape(N_EXPERTS, 1, -1), w2, b2.reshape(N_EXPERTS, 1, -1))


def _moe(x, g, mod, per_row, rows_per_batch, router_w, router_b, w1, b1, w2, b2):
    n, d = x.shape
    h, idx_t, gate_t = _router(x, g, mod, per_row, rows_per_batch, router_w, router_b)
    top_idx = idx_t[:TOP_K].T
    gates = gate_t[:TOP_K].T
    n_asg = n * TOP_K
    tm = 512 if n_asg >= 512 * N_EXPERTS else 64
    flat_e = top_idx.reshape(-1)
    onehot = (flat_e[:, None] == jnp.arange(N_EXPERTS, dtype=jnp.int32)[None, :]).astype(jnp.int32)
    csum = jnp.cumsum(onehot, axis=0)
    rank = jnp.sum(csum * onehot, axis=1) - 1
    counts = csum[-1]
    padded = (counts + tm - 1) // tm * tm
    pad_end = jnp.cumsum(padded)
    pad_start = pad_end - padded
    dest = pad_start[flat_e] + rank
    n_blocks = -(-n_asg // tm) + N_EXPERTS
    rows = n_blocks * tm
    row_tok = jnp.zeros((rows,), jnp.int32).at[dest].set(jnp.arange(n_asg, dtype=jnp.int32) // TOP_K)
    block_e = jnp.minimum(jnp.searchsorted(pad_end, jnp.arange(n_blocks, dtype=jnp.int32) * tm, side='right'),
                          N_EXPERTS - 1).astype(jnp.int32)
    n_used = (pad_end[-1] // tm).astype(jnp.int32).reshape(1)
    xb = h[row_tok]
    yb = _experts(xb, block_e, n_used, w1, b1, w2, b2, tm)
    y = jnp.sum(yb[dest.reshape(n, TOP_K)] * gates[:, :, None], axis=1)
    return y


def _run_group(x3, mod_all, per_row, ssm0, convx0, convc0, cache_k, cache_v, page_table, w):
    (w_in_main, w_in_dt, conv_xbc_w, conv_xbc_b, dt_bias, a_log, d_skip, ssm_norm_g, conf_dw_w, conf_dw_b,
     conf_ln_g, conf_ln_b, w_out_ab, w_qkv, q_norm_g, k_norm_g, sb_bias, w_o, norm_mix_g, norm_ffn_g,
     router_w, router_b, exp_w1, exp_b1, exp_w2, exp_b2) = w
    bsz, l, d = x3.shape
    n = bsz * l
    x = x3.reshape(n, d)
    depth = norm_mix_g.shape[0]
    new_ssm, new_cx, new_cc, new_k, new_v = [], [], [], [], []
    for layer in range(depth):
        mod = mod_all[layer]
        j = layer // 2
        if layer % 2 == 0:
            proj = _norm_mod_matmul(x, norm_mix_g[layer], mod, per_row, l, 1, 0, w_in_main[j], 512)
            dtp = _norm_mod_matmul(x, norm_mix_g[layer], mod, per_row, l, 1, 0, w_in_dt[j], LANES)
            proj3 = proj.reshape(bsz, l, PROJ_W)
            y, s_new, cx_new = _ssd(proj3, dtp.reshape(bsz, l, LANES), ssm0[j], convx0[j], conv_xbc_w[j],
                                    conv_xbc_b[j], dt_bias[j], a_log[j], d_skip[j], ssm_norm_g[j])
            v, cc_new = _conformer(proj3, convc0[j], conf_dw_w[j], conf_dw_b[j], conf_ln_g[j], conf_ln_b[j])
            new_ssm.append(s_new)
            new_cx.append(cx_new)
            new_cc.append(cc_new)
            x = _matmul_residual([y.reshape(n, d), v.reshape(n, d)], w_out_ab[j], x, mod, per_row, l, 2)
        else:
            q, k, kb, v, vb = _qkv_proj(x, norm_mix_g[layer], mod, per_row, l, w_qkv[j], q_norm_g[j], k_norm_g[j])
            if cache_k is None:
                o = _sb_prompt(q.reshape(bsz, l, d), kb.reshape(bsz, l, d), vb.reshape(bsz, l, d), sb_bias[j])
            else:
                o = _sb_decode(q.astype(F32).reshape(bsz, l, d), k.reshape(bsz, l, d), v.reshape(bsz, l, d),
                               cache_k[j], cache_v[j], page_table, sb_bias[j])
            new_k.append(k.reshape(bsz, l, SB_HEADS, SB_HEAD_DIM))
            new_v.append(v.reshape(bsz, l, SB_HEADS, SB_HEAD_DIM))
            x = _matmul_residual([o.reshape(n, d)], w_o[j], x, mod, per_row, l, 2)
        y = _moe(x, norm_ffn_g[layer], mod, per_row, l, router_w[layer], router_b[layer],
                 exp_w1[layer], exp_b1[layer], exp_w2[layer], exp_b2[layer])
        gate_f = mod[:, 5 * d:6 * d] if per_row else jnp.repeat(mod[:, 0, 5 * d:6 * d], l, axis=0)
        x = x + gate_f * y
    return (x.reshape(bsz, l, d), jnp.stack(new_ssm), jnp.stack(new_cx), jnp.stack(new_cc),
            jnp.stack(new_k), jnp.stack(new_v))


def kernel(x_prompt, x_sample, c_prompt, c_sample, state_ssm, state_conv_xbc, state_conv_conf, cache_k, cache_v, page_table, w_in_ab, conv_xbc_w, conv_xbc_b, dt_bias, a_log, d_skip, ssm_norm_g, conf_dw_w, conf_dw_b, conf_ln_g, conf_ln_b, w_out_ab, w_qkv, q_norm_g, k_norm_g, sb_bias, w_o, ada_w, ada_b, norm_mix_g, norm_ffn_g, router_w, router_b, exp_w1, exp_b1, exp_w2, exp_b2):
    bp, lp, d = x_prompt.shape
    bs, ls, _ = x_sample.shape
    n_ab = w_in_ab.shape[0]

    o_z, o_x, o_dt, o_ga = SSM_INNER, SSM_INNER + XBC_DIM, SSM_INNER + XBC_DIM + SSM_HEADS, \
        SSM_INNER + XBC_DIM + SSM_HEADS + CONF_CH
    w_in_main = jnp.concatenate([w_in_ab[:, :, :o_z], w_in_ab[:, :, o_dt:o_ga], w_in_ab[:, :, o_ga:],
                                 w_in_ab[:, :, o_z:o_x]], axis=-1).astype(BF16)
    w_in_dt = jnp.pad(w_in_ab[:, :, o_x:o_dt], ((0, 0), (0, 0), (0, LANES - SSM_HEADS))).astype(BF16)
    weights = (w_in_main, w_in_dt, conv_xbc_w, conv_xbc_b, dt_bias, a_log, d_skip, ssm_norm_g, conf_dw_w, conf_dw_b,
               conf_ln_g, conf_ln_b, w_out_ab.astype(BF16), w_qkv.astype(BF16), q_norm_g, k_norm_g, sb_bias,
               w_o.astype(BF16), norm_mix_g, norm_ffn_g, router_w, router_b, exp_w1, exp_b1, exp_w2, exp_b2)

    n_c = bp + bs
    n_c_pad = -(-n_c // SUBLANES) * SUBLANES
    c_all = jnp.pad(jnp.concatenate([c_prompt, c_sample], axis=0), ((0, n_c_pad - n_c), (0, 0)))
    mod_all = _ada_all(c_all, ada_w, ada_b)
    mod_prompt = mod_all[:, :bp].reshape(mod_all.shape[0], bp, 1, -1)
    mod_sample = jnp.repeat(mod_all[:, bp:n_c], ls, axis=1)

    ssm0 = jnp.zeros((n_ab, bp, SSM_HEADS, SSM_HEAD_DIM, SSM_STATE), F32)
    cx0 = jnp.zeros((n_ab, bp, SSM_CONV - 1, XBC_DIM), F32)
    cc0 = jnp.zeros((n_ab, bp, CONF_WIDTH - 1, CONF_CH), F32)
    y_p, ssm_p, cx_p, cc_p, k_p, v_p = _run_group(x_prompt, mod_prompt, False, ssm0, cx0, cc0, None, None, None,
                                                  weights)
    y_s, ssm_s, cx_s, cc_s, k_s, v_s = _run_group(x_sample, mod_sample, True, state_ssm, state_conv_xbc,
                                                  state_conv_conf, cache_k, cache_v, page_table, weights)
    return (y_p, y_s, ssm_p, cx_p, cc_p, k_p, v_p, ssm_s, cx_s, cc_s, k_s, v_s)
```
